```python
import jax, jax.numpy as jnp
from jax import lax
import numpy as np

D_MODEL = 1024
BATCH = 32
SEQ = 2048
DEPTH = 1

D_MIX = D_MODEL
GLA_HEADS = 4
GLA_HEAD_V = (D_MIX // 2) // GLA_HEADS
GLA_HEAD_K = GLA_HEAD_V // 2
GLA_GATE_RANK = 16
GLA_GATE_TAU = 16.0
GLA_CHUNK = 64
MOBA_HEADS = 4
MOBA_HEAD_DIM = (D_MIX - GLA_HEADS * GLA_HEAD_V) // MOBA_HEADS
MOBA_BLOCK = 256
MOBA_TOPK = 3
MOBA_QUERY_GROUP = 8
ROPE_THETA = 10000.0
D_FF = -(-8 * D_MODEL // (3 * 256)) * 256
EPS = 1e-6

GLA_QK = GLA_HEADS * GLA_HEAD_K
GLA_V = GLA_HEADS * GLA_HEAD_V
MOBA_W = MOBA_HEADS * MOBA_HEAD_DIM
IN_SIZES = (GLA_QK, GLA_QK, GLA_V, GLA_V, GLA_GATE_RANK, MOBA_W, MOBA_W, MOBA_W)
IN_OFFSETS = tuple(int(o) for o in np.cumsum((0,) + IN_SIZES))
D_IN_PROJ = IN_OFFSETS[-1]

kernel_name = "hymba_gla_moba_hybrid_block"


def rms_norm(x, gain):
    xf = x.astype(jnp.float32)
    y = xf * lax.rsqrt(jnp.mean(xf * xf, axis=-1, keepdims=True) + EPS)
    return (y * gain.astype(jnp.float32)).astype(x.dtype)


def rotary(x, positions):
    half = x.shape[-1] // 2
    inv_freq = jnp.power(ROPE_THETA, -jnp.arange(half, dtype=jnp.float32) / half)
    ang = positions.astype(jnp.float32)[:, :, None] * inv_freq
    cos = jnp.cos(ang)[:, None]
    sin = jnp.sin(ang)[:, None]
    xf = x.astype(jnp.float32)
    x1, x2 = xf[..., :half], xf[..., half:]
    return jnp.concatenate([x1 * cos - x2 * sin, x2 * cos + x1 * sin], axis=-1).astype(x.dtype)


def gla_chunked(q, k, v, log_a):
    B, H, S, dk = q.shape
    dv = v.shape[-1]
    nc = S // GLA_CHUNK
    rs = lambda t: t.reshape(B, H, nc, GLA_CHUNK, t.shape[-1])
    q = rs(q) * (dk ** -0.5)
    k = rs(k)
    v = rs(v)
    b = jnp.cumsum(rs(log_a), axis=3)
    b_last = b[:, :, :, -1:, :]
    q_dec = q * jnp.exp(b)
    k_inv = k * jnp.exp(-b)
    causal = jnp.tril(jnp.ones((GLA_CHUNK, GLA_CHUNK), dtype=bool))
    a = jnp.where(causal, jnp.einsum('bhncd,bhnsd->bhncs', q_dec, k_inv), 0.0)
    o_intra = jnp.einsum('bhncs,bhnsv->bhncv', a, v)
    k_to_end = k * jnp.exp(b_last - b)
    u = jnp.einsum('bhncd,bhncv->bhndv', k_to_end, v)
    chunk_decay = jnp.exp(b_last[:, :, :, 0, :])

    def step(state, inp):
        d, u_c = inp
        return d[..., None] * state + u_c, state

    _, s_prev = lax.scan(step, jnp.zeros((B, H, dk, dv), jnp.float32),
                         (jnp.moveaxis(chunk_decay, 2, 0), jnp.moveaxis(u, 2, 0)))
    s_prev = jnp.moveaxis(s_prev, 0, 2)
    o_inter = jnp.einsum('bhncd,bhndv->bhncv', q_dec, s_prev)
    return (o_intra + o_inter).reshape(B, H, S, dv)


def moba_attention(q, k, v):
    B, H, S, dh = q.shape
    nb = -(-S // MOBA_BLOCK)
    s_pad = nb * MOBA_BLOCK
    pad = ((0, 0), (0, 0), (0, s_pad - S), (0, 0))
    q_p, k_p, v_p = jnp.pad(q, pad), jnp.pad(k, pad), jnp.pad(v, pad)
    kb = k_p.reshape(B, H, nb, MOBA_BLOCK, dh)
    vb = v_p.reshape(B, H, nb, MOBA_BLOCK, dh)
    k_mean = jnp.mean(kb.astype(jnp.float32), axis=3)
    gate = jnp.einsum('bhsd,bhnd->bhsn', q_p.astype(jnp.float32), k_mean)
    q_blk = jnp.arange(s_pad) // MOBA_BLOCK
    past = jnp.arange(nb)[None, :] < q_blk[:, None]
    gate = jnp.where(past, gate, -jnp.inf)
    top = min(MOBA_TOPK, nb)
    _, sel = lax.top_k(gate, top)
    sel_valid = sel < q_blk[None, None, :, None]
    scale = dh ** -0.5
    bi = jnp.arange(B)[:, None, None, None]
    hi = jnp.arange(H)[None, :, None, None]
    G = MOBA_QUERY_GROUP

    def group(gi):
        start = gi * G
        qg = lax.dynamic_slice_in_dim(q_p, start, G, axis=2)
        selg = lax.dynamic_slice_in_dim(sel, start, G, axis=2)
        validg = lax.dynamic_slice_in_dim(sel_valid, start, G, axis=2)
        k_sel = kb[bi, hi, selg]
        v_sel = vb[bi, hi, selg]
        s_sel = jnp.einsum('bhgd,bhgnkd->bhgnk', qg, k_sel).astype(jnp.float32) * scale
        s_sel = jnp.where(validg[..., None], s_sel, -jnp.inf).reshape(B, H, G, top * MOBA_BLOCK)
        own = start // MOBA_BLOCK
        k_own = lax.dynamic_slice_in_dim(k_p, own * MOBA_BLOCK, MOBA_BLOCK, axis=2)
        v_own = lax.dynamic_slice_in_dim(v_p, own * MOBA_BLOCK, MOBA_BLOCK, axis=2)
        s_own = jnp.einsum('bhgd,bhkd->bhgk', qg, k_own).astype(jnp.float32) * scale
        qpos = start + jnp.arange(G)
        kpos = own * MOBA_BLOCK + jnp.arange(MOBA_BLOCK)
        s_own = jnp.where(kpos[None, :] <= qpos[:, None], s_own, -jnp.inf)
        p = jax.nn.softmax(jnp.concatenate([s_sel, s_own], axis=-1), axis=-1)
        p_sel = p[..., :top * MOBA_BLOCK].reshape(B, H, G, top, MOBA_BLOCK).astype(v.dtype)
        p_own = p[..., top * MOBA_BLOCK:].astype(v.dtype)
        return (jnp.einsum('bhgnk,bhgnkd->bhgd', p_sel, v_sel)
                + jnp.einsum('bhgk,bhkd->bhgd', p_own, v_own))

    out = lax.map(group, jnp.arange(s_pad // G))
    out = jnp.transpose(out, (1, 2, 0, 3, 4)).reshape(B, H, s_pad, dh)
    return out[:, :, :S]


def setup_inputs(seed: int = 0) -> dict:
    key = jax.random.key(seed)
    ks = jax.random.split(key, 16)
    f32 = jnp.float32
    nrm = lambda k, shape, fan: jax.random.normal(k, shape, f32) * (fan ** -0.5)
    gain = lambda k, shape: 1.0 + 0.02 * jax.random.normal(k, shape, f32)
    x = jax.random.normal(ks[0], (BATCH, SEQ, D_MODEL), f32)
    offsets = jax.random.randint(ks[1], (BATCH, 1), 0, 1024, dtype=jnp.int32)
    positions = (offsets + jnp.arange(SEQ, dtype=jnp.int32)[None, :]).astype(jnp.int32)
    return {
        "x": x,
        "positions": positions,
        "norm1_gain": gain(ks[2], (DEPTH, D_MODEL)),
        "w_in": nrm(ks[3], (DEPTH, D_MODEL, D_IN_PROJ), D_MODEL),
        "w_gate_up": nrm(ks[4], (DEPTH, GLA_GATE_RANK, GLA_QK), GLA_GATE_RANK),
        "b_gate": 0.1 * jax.random.normal(ks[5], (DEPTH, GLA_QK), f32),
        "gla_out_gain": gain(ks[6], (DEPTH, GLA_HEAD_V)),
        "moba_out_gain": gain(ks[7], (DEPTH, MOBA_HEAD_DIM)),
        "w_out": nrm(ks[8], (DEPTH, D_MIX, D_MODEL), D_MIX),
        "norm2_gain": gain(ks[9], (DEPTH, D_MODEL)),
        "w_ffn_gate": nrm(ks[10], (DEPTH, D_MODEL, D_FF), D_MODEL),
        "w_ffn_up": nrm(ks[11], (DEPTH, D_MODEL, D_FF), D_MODEL),
        "w_ffn_down": nrm(ks[12], (DEPTH, D_FF, D_MODEL), D_FF),
        "final_norm_gain": gain(ks[13], (D_MODEL,)),
    }


def reference(x, positions, norm1_gain, w_in, w_gate_up, b_gate, gla_out_gain, moba_out_gain,
              w_out, norm2_gain, w_ffn_gate, w_ffn_up, w_ffn_down, final_norm_gain):
    B, S, _ = x.shape
    dt = x.dtype
    to_heads = lambda t, h: t.reshape(B, S, h, -1).transpose(0, 2, 1, 3)
    for l in range(DEPTH):
        h = rms_norm(x, norm1_gain[l])
        proj = h @ w_in[l]
        parts = [proj[..., IN_OFFSETS[i]:IN_OFFSETS[i + 1]] for i in range(len(IN_SIZES))]
        gq, gk, gv, gr, g_low, mq, mk, mv = parts
        gate_logit = (g_low @ w_gate_up[l] + b_gate[l]).astype(jnp.float32)
        log_a = jax.nn.log_sigmoid(gate_logit) / GLA_GATE_TAU
        o_gla = gla_chunked(to_heads(gq, GLA_HEADS).astype(jnp.float32),
                            to_heads(gk, GLA_HEADS).astype(jnp.float32),
                            to_heads(gv, GLA_HEADS).astype(jnp.float32),
                            to_heads(log_a, GLA_HEADS)).astype(dt)
        o_gla = rms_norm(o_gla.transpose(0, 2, 1, 3), gla_out_gain[l])
        o_gla = (o_gla * jax.nn.silu(gr.reshape(B, S, GLA_HEADS, GLA_HEAD_V))).reshape(B, S, GLA_V)
        q_m = rotary(to_heads(mq, MOBA_HEADS), positions)
        k_m = rotary(to_heads(mk, MOBA_HEADS), positions)
        o_moba = moba_attention(q_m, k_m, to_heads(mv, MOBA_HEADS))
        o_moba = rms_norm(o_moba.transpose(0, 2, 1, 3), moba_out_gain[l]).reshape(B, S, MOBA_W)
        x = x + jnp.concatenate([o_gla, o_moba], axis=-1) @ w_out[l]
        h = rms_norm(x, norm2_gain[l])
        x = x + (jax.nn.silu(h @ w_ffn_gate[l]) * (h @ w_ffn_up[l])) @ w_ffn_down[l]
    return rms_norm(x, final_norm_gain)
```

```python
import functools

import jax
import jax.numpy as jnp
import numpy as np
from jax import lax
from jax.experimental import pallas as pl
from jax.experimental.pallas import tpu as pltpu

GLA_HEADS = 4
GLA_GATE_RANK = 16
GLA_GATE_TAU = 16.0
MOBA_HEADS = 4
MOBA_BLOCK = 256
MOBA_TOPK = 3
ROPE_THETA = 10000.0
EPS = 1e-6

LANES = 128
VMEM_LIMIT_BYTES = 56 * 1024 * 1024

GLA_KERNEL_CHUNK = 128
TOKEN_TILE = 512
FFN_CHUNK = 512

MASK_VALUE = -1e30

F32 = jnp.float32
BF16 = jnp.bfloat16


def _dot(a, b, precision=None):
    return jnp.dot(a, b, preferred_element_type=F32, precision=precision)


def _dot_nt(a, b):
    return lax.dot_general(a, b, (((1,), (1,)), ((), ())), preferred_element_type=F32)


def _rms(x, gain):
    ms = jnp.mean(x * x, axis=-1, keepdims=True)
    return x * lax.rsqrt(ms + EPS) * gain


def _resident(shape):
    nd = len(shape)
    return pl.BlockSpec(shape, lambda *_: (0,) * nd, pipeline_mode=pl.Buffered(1))


def _in_proj_kernel(dims, x_ref, pos_ref, g1_ref, w_ref, wkT_ref, wglT_ref, wgu_ref, wguT_ref,
                    bg_ref, bgT_ref, invf_ref,
                    q_ref, kT_ref, v_ref, r_ref, la_ref, laT_ref, mq_ref, mk_ref, mv_ref):
    gla_qk, gla_v, moba_w, moba_dh = dims
    h = _rms(x_ref[...], g1_ref[...]).astype(BF16)

    def proj(lo, width):
        return _dot(h, w_ref[:, lo:lo + width])

    off = 0
    q_ref[...] = proj(off, gla_qk).astype(BF16)
    off += gla_qk
    v_ref[...] = proj(off, gla_v).astype(BF16)
    off += gla_v
    gr = proj(off, gla_v)
    r_ref[...] = (gr / (1.0 + jnp.exp(-gr))).astype(BF16)
    off += gla_v
    mq = proj(off, moba_w)
    off += moba_w
    mk = proj(off, moba_w)
    off += moba_w
    mv_ref[...] = proj(off, moba_w).astype(BF16)
    off += moba_w
    g_low = proj(off, LANES).astype(BF16)

    kT_ref[...] = _dot_nt(wkT_ref[...], h).astype(BF16)
    g_lowT = _dot_nt(wglT_ref[...], h).astype(BF16)

    def log_decay(logit):
        log_sig = jnp.minimum(logit, 0.0) - jnp.log1p(jnp.exp(-jnp.abs(logit)))
        return log_sig / GLA_GATE_TAU

    la_ref[...] = log_decay(_dot(g_low, wgu_ref[...]) + bg_ref[...])
    laT_ref[...] = log_decay(_dot(wguT_ref[...], g_lowT) + bgT_ref[...])

    ang = pos_ref[...].astype(F32) * invf_ref[...]
    cos = jnp.cos(ang)
    sin = jnp.sin(ang)
    lane = lax.broadcasted_iota(jnp.int32, ang.shape, 1)
    sin_signed = jnp.where(lane < moba_dh // 2, -sin, sin)
    scale = moba_dh ** -0.5
    for hh in range(moba_w // moba_dh):
        sl = slice(hh * moba_dh, (hh + 1) * moba_dh)
        xq = mq[:, sl]
        xk = mk[:, sl]
        rq = xq * cos + pltpu.roll(xq, moba_dh // 2, 1) * sin_signed
        rk = xk * cos + pltpu.roll(xk, moba_dh // 2, 1) * sin_signed
        mq_ref[:, sl] = (rq * scale).astype(BF16)
        mk_ref[:, sl] = rk.astype(BF16)


def _in_proj(x2d, pos2d, g1, w_tok, w_kT, w_glT, w_gu, w_guT, b_g, b_gT, inv_freq, dims):
    gla_qk, gla_v, moba_w, moba_dh = dims
    T, D = x2d.shape
    tm = TOKEN_TILE
    row = lambda width: pl.BlockSpec((tm, width), lambda i: (i, 0))
    col = lambda height: pl.BlockSpec((height, tm), lambda i: (0, i))
    out_shapes = (
        jax.ShapeDtypeStruct((T, gla_qk), BF16),
        jax.ShapeDtypeStruct((gla_qk, T), BF16),
        jax.ShapeDtypeStruct((T, gla_v), BF16),
        jax.ShapeDtypeStruct((T, gla_v), BF16),
        jax.ShapeDtypeStruct((T, gla_qk), F32),
        jax.ShapeDtypeStruct((gla_qk, T), F32),
        jax.ShapeDtypeStruct((T, moba_w), BF16),
        jax.ShapeDtypeStruct((T, moba_w), BF16),
        jax.ShapeDtypeStruct((T, moba_w), BF16),
    )
    out_specs = (row(gla_qk), col(gla_qk), row(gla_v), row(gla_v), row(gla_qk), col(gla_qk),
                 row(moba_w), row(moba_w), row(moba_w))
    in_specs = [row(D), row(1), _resident(g1.shape), _resident(w_tok.shape), _resident(w_kT.shape),
                _resident(w_glT.shape), _resident(w_gu.shape), _resident(w_guT.shape),
                _resident(b_g.shape), _resident(b_gT.shape), _resident(inv_freq.shape)]
    return pl.pallas_call(
        functools.partial(_in_proj_kernel, dims),
        grid=(T // tm,),
        in_specs=in_specs,
        out_specs=out_specs,
        out_shape=out_shapes,
        compiler_params=pltpu.CompilerParams(dimension_semantics=("arbitrary",),
                                             vmem_limit_bytes=VMEM_LIMIT_BYTES),
        name="in_proj",
    )(x2d, pos2d, g1, w_tok, w_kT, w_glT, w_gu, w_guT, b_g, b_gT, inv_freq)


def _gla_kernel(dims, q_ref, kT_ref, v_ref, la_ref, laT_ref, r_ref, gain_ref, o_ref, state_ref):
    heads, dk, dv = dims
    S = q_ref.shape[0]
    C = GLA_KERNEL_CHUNK
    mid = C // 2 - 1
    hi = lax.Precision.HIGHEST

    r_i = lax.broadcasted_iota(jnp.int32, (C, C), 0)
    c_i = lax.broadcasted_iota(jnp.int32, (C, C), 1)
    causal = r_i >= c_i
    tril = causal.astype(F32)
    triu = (r_i <= c_i).astype(F32)

    sr = lax.broadcasted_iota(jnp.int32, (heads * dk, heads * dv), 0) // dk
    sc = lax.broadcasted_iota(jnp.int32, (heads * dk, heads * dv), 1) // dv
    same_head = sr == sc
    feat_head = lax.broadcasted_iota(jnp.int32, (C, heads * dk), 1) // dk

    state_ref[...] = jnp.zeros_like(state_ref)
    scale = dk ** -0.5
    gain = gain_ref[...]

    def step(n, carry):
        r0 = pl.multiple_of(n * C, C)
        la = la_ref[pl.ds(r0, C), :]
        laT = laT_ref[:, pl.ds(r0, C)]
        b = _dot(tril, la, hi)
        bT = _dot(laT, triu, hi)
        b_mid = b[mid:mid + 1, :]
        bT_mid = bT[:, mid:mid + 1]
        bT_last = bT[:, C - 1:C]

        q = q_ref[pl.ds(r0, C), :].astype(F32) * scale
        kT = kT_ref[:, pl.ds(r0, C)].astype(F32)
        v = v_ref[pl.ds(r0, C), :]

        q_inter = (q * jnp.exp(b)).astype(BF16)
        q_intra = q * jnp.exp(b - b_mid)
        k_intraT = (kT * jnp.exp(bT_mid - bT)).astype(BF16)
        k_endT = (kT * jnp.exp(bT_last - bT)).astype(BF16)

        state = state_ref[...]
        o = _dot(q_inter, state.astype(BF16))
        u = _dot(k_endT, v)
        decay = jnp.exp(bT_last)
        state_ref[...] = decay * state + jnp.where(same_head, u, 0.0)

        for hh in range(heads):
            qh = jnp.where(feat_head == hh, q_intra, 0.0).astype(BF16)
            a = _dot(qh, k_intraT)
            a = jnp.where(causal, a, 0.0).astype(BF16)
            sl = slice(hh * dv, (hh + 1) * dv)
            oh = o[:, sl] + _dot(a, v[:, sl])
            y = _rms(oh, gain) * r_ref[pl.ds(r0, C), sl].astype(F32)
            o_ref[pl.ds(r0, C), sl] = y.astype(o_ref.dtype)
        return carry

    lax.fori_loop(0, S // C, step, 0)


def _gla(q, kT, v, la, laT, r, gain, batch, dims):
    heads, dk, dv = dims
    T = q.shape[0]
    S = T // batch
    row = lambda width: pl.BlockSpec((S, width), lambda b: (b, 0))
    col = lambda height: pl.BlockSpec((height, S), lambda b: (0, b))
    return pl.pallas_call(
        functools.partial(_gla_kernel, dims),
        grid=(batch,),
        in_specs=[row(heads * dk), col(heads * dk), row(heads * dv), row(heads * dk),
                  col(heads * dk), row(heads * dv), _resident(gain.shape)],
        out_specs=row(heads * dv),
        out_shape=jax.ShapeDtypeStruct((T, heads * dv), BF16),
        scratch_shapes=[pltpu.VMEM((heads * dk, heads * dv), F32)],
        compiler_params=pltpu.CompilerParams(dimension_semantics=("arbitrary",),
                                             vmem_limit_bytes=VMEM_LIMIT_BYTES),
        name="gla",
    )(q, kT, v, la, laT, r, gain)


def _moba_kernel(q_ref, k_ref, v_ref, gain_ref, o_ref):
    S, dh = q_ref.shape
    blk = MOBA_BLOCK
    nb = S // blk
    q = q_ref[...]
    k = k_ref[...]
    v = v_ref[...]
    gain = gain_ref[...]

    pr = lax.broadcasted_iota(jnp.int32, (nb, S), 0)
    pc = lax.broadcasted_iota(jnp.int32, (nb, S), 1) // blk
    pool = jnp.where(pr == pc, 1.0 / blk, 0.0).astype(BF16)
    k_mean = _dot(pool, k)
    km_hi = k_mean.astype(BF16)
    km_lo = (k_mean - km_hi.astype(F32)).astype(BF16)
    g2 = _dot_nt(jnp.concatenate([km_hi, km_lo], axis=0), q)
    gate = g2[:nb] + g2[nb:]

    blk_row = lax.broadcasted_iota(jnp.int32, (nb, S), 0)
    q_blk = lax.broadcasted_iota(jnp.int32, (nb, S), 1) // blk
    is_past = blk_row < q_blk
    bias_rows = []
    for j in range(nb):
        gj = gate[j:j + 1, :]
        beats = (gate > gj) | ((gate == gj) & (blk_row < j))
        rank = jnp.sum(jnp.where(beats & is_past, 1.0, 0.0), axis=0, keepdims=True)
        keep = (rank < MOBA_TOPK) & (q_blk[0:1, :] > j)
        bias_rows.append(jnp.where(keep, 0.0, MASK_VALUE))
    pad_rows = jnp.full((LANES - nb, S), 0.0, F32)
    biasT = jnp.concatenate(bias_rows + [pad_rows], axis=0).astype(BF16)

    er = lax.broadcasted_iota(jnp.int32, (blk, blk), 0)
    ec = lax.broadcasted_iota(jnp.int32, (blk, blk), 1)
    eye = (er == ec).astype(BF16)
    own_ok = ec <= er

    for i in range(nb):
        rows = slice(i * blk, (i + 1) * blk)
        nk = (i + 1) * blk
        s = _dot_nt(q[rows], k[:nk])
        bias_q = _dot_nt(eye, biasT[:, rows])
        parts = [s[:, j * blk:(j + 1) * blk] + bias_q[:, j:j + 1] for j in range(i)]
        parts.append(jnp.where(own_ok, s[:, i * blk:], MASK_VALUE))
        s = jnp.concatenate(parts, axis=1) if i else parts[0]
        m = jnp.max(s, axis=1, keepdims=True)
        p = jnp.exp(s - m)
        l = jnp.sum(p, axis=1, keepdims=True)
        o = _dot(p.astype(BF16), v[:nk]) / l
        o_ref[rows, :] = _rms(o, gain).astype(o_ref.dtype)


def _moba(mq, mk, mv, gain, batch, heads):
    T, width = mq.shape
    dh = width // heads
    S = T // batch
    spec = pl.BlockSpec((S, dh), lambda b, h: (b, h))
    return pl.pallas_call(
        _moba_kernel,
        grid=(batch, heads),
        in_specs=[spec, spec, spec, _resident(gain.shape)],
        out_specs=spec,
        out_shape=jax.ShapeDtypeStruct((T, width), BF16),
        compiler_params=pltpu.CompilerParams(dimension_semantics=("arbitrary", "arbitrary"),
                                             vmem_limit_bytes=VMEM_LIMIT_BYTES),
        name="moba",
    )(mq, mk, mv, gain)


def _out_ffn_kernel(apply_final, x_ref, og_ref, om_ref, wo_ref, g2_ref, wg_ref, wu_ref, wd_ref,
                    gf_ref, y_ref):
    n_gla = og_ref.shape[1]
    x1 = x_ref[...] + _dot(og_ref[...], wo_ref[:n_gla, :]) + _dot(om_ref[...], wo_ref[n_gla:, :])
    h = _rms(x1, g2_ref[...]).astype(BF16)
    d_ff = wg_ref.shape[1]
    ffn = None
    for lo in range(0, d_ff, FFN_CHUNK):
        width = min(FFN_CHUNK, d_ff - lo)
        g = _dot(h, wg_ref[:, lo:lo + width])
        u = _dot(h, wu_ref[:, lo:lo + width])
        act = (g / (1.0 + jnp.exp(-g)) * u).astype(BF16)
        part = _dot(act, wd_ref[lo:lo + width, :])
        ffn = part if ffn is None else ffn + part
    acc = x1 + ffn
    if apply_final:
        acc = _rms(acc, gf_ref[...])
    y_ref[...] = acc


def _out_ffn(x2d, o_gla, o_moba, w_out, g2, w_gate, w_up, w_down, g_final, apply_final):
    T, D = x2d.shape
    tm = TOKEN_TILE
    row = lambda width: pl.BlockSpec((tm, width), lambda i: (i, 0))
    return pl.pallas_call(
        functools.partial(_out_ffn_kernel, apply_final),
        grid=(T // tm,),
        in_specs=[row(D), row(o_gla.shape[1]), row(o_moba.shape[1]), _resident(w_out.shape),
                  _resident(g2.shape), _resident(w_gate.shape), _resident(w_up.shape),
                  _resident(w_down.shape), _resident(g_final.shape)],
        out_specs=row(D),
        out_shape=jax.ShapeDtypeStruct((T, D), F32),
        compiler_params=pltpu.CompilerParams(dimension_semantics=("arbitrary",),
                                             vmem_limit_bytes=VMEM_LIMIT_BYTES),
        name="out_ffn",
    )(x2d, o_gla, o_moba, w_out, g2, w_gate, w_up, w_down, g_final)


def kernel(x, positions, norm1_gain, w_in, w_gate_up, b_gate, gla_out_gain, moba_out_gain,
           w_out, norm2_gain, w_ffn_gate, w_ffn_up, w_ffn_down, final_norm_gain):
    B, S, D = x.shape
    depth = w_in.shape[0]
    gla_v = D // 2
    gla_dv = gla_v // GLA_HEADS
    gla_dk = gla_dv // 2
    gla_qk = GLA_HEADS * gla_dk
    moba_w = D - gla_v
    moba_dh = moba_w // MOBA_HEADS
    assert S % MOBA_BLOCK == 0 and S % GLA_KERNEL_CHUNK == 0 and (B * S) % TOKEN_TILE == 0
    assert moba_dh == LANES and gla_dv == LANES and GLA_GATE_RANK <= LANES

    sizes = (gla_qk, gla_qk, gla_v, gla_v, GLA_GATE_RANK, moba_w, moba_w, moba_w)
    offs = np.cumsum((0,) + sizes)
    seg = lambda w, i: w[:, offs[i]:offs[i + 1]]

    half = moba_dh // 2
    inv_freq = jnp.power(ROPE_THETA, -jnp.arange(half, dtype=F32) / half)
    inv_freq = jnp.concatenate([inv_freq, inv_freq])[None, :]

    T = B * S
    x2d = x.reshape(T, D)
    pos2d = positions.reshape(T, 1)
    pad_rank = LANES - GLA_GATE_RANK

    for l in range(depth):
        w = w_in[l]
        w_tok = jnp.concatenate(
            [seg(w, 0), seg(w, 2), seg(w, 3), seg(w, 5), seg(w, 6), seg(w, 7),
             jnp.pad(seg(w, 4), ((0, 0), (0, pad_rank)))], axis=1).astype(BF16)
        w_kT = seg(w, 1).T.astype(BF16)
        w_glT = jnp.pad(seg(w, 4), ((0, 0), (0, pad_rank))).T.astype(BF16)
        w_gu = jnp.pad(w_gate_up[l], ((0, pad_rank), (0, 0))).astype(BF16)
        w_guT = w_gu.T
        b_g = b_gate[l][None, :]
        b_gT = b_gate[l][:, None]

        q, kT, v, r, la, laT, mq, mk, mv = _in_proj(
            x2d, pos2d, norm1_gain[l][None, :], w_tok, w_kT, w_glT, w_gu, w_guT, b_g, b_gT,
            inv_freq, (gla_qk, gla_v, moba_w, moba_dh))
        o_gla = _gla(q, kT, v, la, laT, r, gla_out_gain[l][None, :], B, (GLA_HEADS, gla_dk, gla_dv))
        o_moba = _moba(mq, mk, mv, moba_out_gain[l][None, :], B, MOBA_HEADS)
        x2d = _out_ffn(x2d, o_gla, o_moba, w_out[l].astype(BF16), norm2_gain[l][None, :],
                       w_ffn_gate[l].astype(BF16), w_ffn_up[l].astype(BF16),
                       w_ffn_down[l].astype(BF16), final_norm_gain[None, :], l == depth - 1)
    return x2d.reshape(B, S, D)
```

```python
import functools

import jax
import jax.numpy as jnp
import numpy as np
from jax import lax
from jax.experimental import pallas as pl
from jax.experimental.pallas import tpu as pltpu

GLA_HEADS = 4
GLA_GATE_RANK = 16
GLA_GATE_TAU = 16.0
MOBA_HEADS = 4
MOBA_BLOCK = 256
MOBA_TOPK = 3
ROPE_THETA = 10000.0
EPS = 1e-6

LANES = 128
VMEM_LIMIT_BYTES = 56 * 1024 * 1024

GLA_KERNEL_CHUNK = 128
GLA_UNROLL = 4
TOKEN_TILE = 512
FFN_CHUNK = 512

MASK_VALUE = -1e30

F32 = jnp.float32
BF16 = jnp.bfloat16


def _dot(a, b, precision=None):
    return jnp.dot(a, b, preferred_element_type=F32, precision=precision)


def _dot_nt(a, b):
    return lax.dot_general(a, b, (((1,), (1,)), ((), ())), preferred_element_type=F32)


def _rms(x, gain):
    ms = jnp.mean(x * x, axis=-1, keepdims=True)
    return x * lax.rsqrt(ms + EPS) * gain


def _resident(shape):
    nd = len(shape)
    return pl.BlockSpec(shape, lambda *_: (0,) * nd, pipeline_mode=pl.Buffered(1))


def _in_proj_kernel(dims, x_ref, pos_ref, g1_ref, w_ref, wkT_ref, wglT_ref, wgu_ref, wguT_ref,
                    bg_ref, bgT_ref, invf_ref,
                    q_ref, kT_ref, v_ref, r_ref, la_ref, laT_ref, mq_ref, mk_ref, mv_ref):
    gla_qk, gla_v, moba_w, moba_dh = dims
    h = _rms(x_ref[...], g1_ref[...]).astype(BF16)

    def proj(lo, width):
        return _dot(h, w_ref[:, lo:lo + width])

    off = 0
    gla_dk = gla_qk // GLA_HEADS
    q_ref[...] = (proj(off, gla_qk) * gla_dk ** -0.5).astype(BF16)
    off += gla_qk
    v_ref[...] = proj(off, gla_v).astype(BF16)
    off += gla_v
    gr = proj(off, gla_v)
    r_ref[...] = (gr / (1.0 + jnp.exp(-gr))).astype(BF16)
    off += gla_v
    mq = proj(off, moba_w)
    off += moba_w
    mk = proj(off, moba_w)
    off += moba_w
    mv_ref[...] = proj(off, moba_w).astype(BF16)
    off += moba_w
    g_low = proj(off, LANES).astype(BF16)

    kT_ref[...] = _dot_nt(wkT_ref[...], h).astype(BF16)
    g_lowT = _dot_nt(wglT_ref[...], h).astype(BF16)

    def log_decay(logit):
        log_sig = jnp.minimum(logit, 0.0) - jnp.log1p(jnp.exp(-jnp.abs(logit)))
        return log_sig / GLA_GATE_TAU

    la_ref[...] = log_decay(_dot(g_low, wgu_ref[...]) + bg_ref[...])
    laT_ref[...] = log_decay(_dot(wguT_ref[...], g_lowT) + bgT_ref[...])

    ang = pos_ref[...].astype(F32) * invf_ref[...]
    cos = jnp.cos(ang)
    sin = jnp.sin(ang)
    lane = lax.broadcasted_iota(jnp.int32, ang.shape, 1)
    sin_signed = jnp.where(lane < moba_dh // 2, -sin, sin)
    scale = moba_dh ** -0.5
    for hh in range(moba_w // moba_dh):
        sl = slice(hh * moba_dh, (hh + 1) * moba_dh)
        xq = mq[:, sl]
        xk = mk[:, sl]
        rq = xq * cos + pltpu.roll(xq, moba_dh // 2, 1) * sin_signed
        rk = xk * cos + pltpu.roll(xk, moba_dh // 2, 1) * sin_signed
        mq_ref[:, sl] = (rq * scale).astype(BF16)
        mk_ref[:, sl] = rk.astype(BF16)


def _in_proj(x2d, pos2d, g1, w_tok, w_kT, w_glT, w_gu, w_guT, b_g, b_gT, inv_freq, dims):
    gla_qk, gla_v, moba_w, moba_dh = dims
    T, D = x2d.shape
    tm = TOKEN_TILE
    row = lambda width: pl.BlockSpec((tm, width), lambda i: (i, 0))
    col = lambda height: pl.BlockSpec((height, tm), lambda i: (0, i))
    out_shapes = (
        jax.ShapeDtypeStruct((T, gla_qk), BF16),
        jax.ShapeDtypeStruct((gla_qk, T), BF16),
        jax.ShapeDtypeStruct((T, gla_v), BF16),
        jax.ShapeDtypeStruct((T, gla_v), BF16),
        jax.ShapeDtypeStruct((T, gla_qk), F32),
        jax.ShapeDtypeStruct((gla_qk, T), F32),
        jax.ShapeDtypeStruct((T, moba_w), BF16),
        jax.ShapeDtypeStruct((T, moba_w), BF16),
        jax.ShapeDtypeStruct((T, moba_w), BF16),
    )
    out_specs = (row(gla_qk), col(gla_qk), row(gla_v), row(gla_v), row(gla_qk), col(gla_qk),
                 row(moba_w), row(moba_w), row(moba_w))
    in_specs = [row(D), row(1), _resident(g1.shape), _resident(w_tok.shape), _resident(w_kT.shape),
                _resident(w_glT.shape), _resident(w_gu.shape), _resident(w_guT.shape),
                _resident(b_g.shape), _resident(b_gT.shape), _resident(inv_freq.shape)]
    return pl.pallas_call(
        functools.partial(_in_proj_kernel, dims),
        grid=(T // tm,),
        in_specs=in_specs,
        out_specs=out_specs,
        out_shape=out_shapes,
        compiler_params=pltpu.CompilerParams(dimension_semantics=("arbitrary",),
                                             vmem_limit_bytes=VMEM_LIMIT_BYTES),
        name="in_proj",
    )(x2d, pos2d, g1, w_tok, w_kT, w_glT, w_gu, w_guT, b_g, b_gT, inv_freq)


def _gla_kernel(dims, q_ref, kT_ref, v_ref, la_ref, laT_ref, r_ref, gain_ref, o_ref,
                state_ref, state_bf_ref, kz_ref):
    heads, dk, dv = dims
    S = q_ref.shape[0]
    C = GLA_KERNEL_CHUNK
    mid = C // 2 - 1

    r_i = lax.broadcasted_iota(jnp.int32, (C, C), 0)
    c_i = lax.broadcasted_iota(jnp.int32, (C, C), 1)
    causal = r_i >= c_i
    tril = causal.astype(BF16)
    triu = (r_i <= c_i).astype(BF16)

    state_ref[...] = jnp.zeros_like(state_ref)
    state_bf_ref[...] = jnp.zeros_like(state_bf_ref)
    kz_ref[...] = jnp.zeros_like(kz_ref)
    gain = gain_ref[...]

    def split(a):
        a_hi = a.astype(BF16)
        return a_hi, (a - a_hi.astype(F32)).astype(BF16)

    def step(n, carry):
        r0 = pl.multiple_of(n * C, C)
        la_hi, la_lo = split(la_ref[pl.ds(r0, C), :])
        laT_hi, laT_lo = split(laT_ref[:, pl.ds(r0, C)])
        b = _dot(tril, la_hi) + _dot(tril, la_lo)
        bT = _dot(laT_hi, triu) + _dot(laT_lo, triu)
        b_mid = b[mid:mid + 1, :]
        bT_mid = bT[:, mid:mid + 1]
        bT_last = bT[:, C - 1:C]

        q = q_ref[pl.ds(r0, C), :].astype(F32)
        kT = kT_ref[:, pl.ds(r0, C)].astype(F32)
        q_inter = (q * jnp.exp(b)).astype(BF16)
        q_intra = (q * jnp.exp(b - b_mid)).astype(BF16)
        k_intraT = (kT * jnp.exp(bT_mid - bT)).astype(BF16)
        k_endT = (kT * jnp.exp(bT_last - bT)).astype(BF16)
        decay = jnp.exp(bT_last)

        o_inter = _dot(q_inter, state_bf_ref[...])
        for hh in range(heads):
            kr = slice(hh * dk, (hh + 1) * dk)
            vc = slice(hh * dv, (hh + 1) * dv)
            kz_ref[hh, kr, :] = k_intraT[kr, :]
            a = _dot(q_intra, kz_ref[hh])
            a = jnp.where(causal, a, 0.0).astype(BF16)
            vh = v_ref[pl.ds(r0, C), vc]
            oh = o_inter[:, vc] + _dot(a, vh)
            y = _rms(oh, gain) * r_ref[pl.ds(r0, C), vc].astype(F32)
            o_ref[pl.ds(r0, C), vc] = y.astype(o_ref.dtype)
            s_new = decay[kr, :] * state_ref[hh] + _dot(k_endT[kr, :], vh)
            state_ref[hh] = s_new
            state_bf_ref[kr, vc] = s_new.astype(BF16)
        return carry

    lax.fori_loop(0, S // C, step, 0, unroll=GLA_UNROLL)


def _gla(q, kT, v, la, laT, r, gain, batch, dims):
    heads, dk, dv = dims
    T = q.shape[0]
    S = T // batch
    row = lambda width: pl.BlockSpec((S, width), lambda b: (b, 0))
    col = lambda height: pl.BlockSpec((height, S), lambda b: (0, b))
    return pl.pallas_call(
        functools.partial(_gla_kernel, dims),
        grid=(batch,),
        in_specs=[row(heads * dk), col(heads * dk), row(heads * dv), row(heads * dk),
                  col(heads * dk), row(heads * dv), _resident(gain.shape)],
        out_specs=row(heads * dv),
        out_shape=jax.ShapeDtypeStruct((T, heads * dv), BF16),
        scratch_shapes=[pltpu.VMEM((heads, dk, dv), F32),
                        pltpu.VMEM((heads * dk, heads * dv), BF16),
                        pltpu.VMEM((heads, heads * dk, GLA_KERNEL_CHUNK), BF16)],
        compiler_params=pltpu.CompilerParams(dimension_semantics=("arbitrary",),
                                             vmem_limit_bytes=VMEM_LIMIT_BYTES),
        name="gla",
    )(q, kT, v, la, laT, r, gain)


def _moba_kernel(q_ref, k_ref, v_ref, gain_ref, o_ref):
    S, dh = q_ref.shape
    blk = MOBA_BLOCK
    nb = S // blk
    q = q_ref[...]
    k = k_ref[...]
    v = v_ref[...]
    gain = gain_ref[...]

    pr = lax.broadcasted_iota(jnp.int32, (nb, S), 0)
    pc = lax.broadcasted_iota(jnp.int32, (nb, S), 1) // blk
    pool = jnp.where(pr == pc, 1.0 / blk, 0.0).astype(BF16)
    k_mean = _dot(pool, k)
    km_hi = k_mean.astype(BF16)
    km_lo = (k_mean - km_hi.astype(F32)).astype(BF16)
    g2 = _dot_nt(jnp.concatenate([km_hi, km_lo], axis=0), q)
    gate = g2[:nb] + g2[nb:]

    blk_row = lax.broadcasted_iota(jnp.int32, (nb, S), 0)
    q_blk = lax.broadcasted_iota(jnp.int32, (nb, S), 1) // blk
    is_past = blk_row < q_blk
    bias_rows = []
    for j in range(nb):
        gj = gate[j:j + 1, :]
        beats = (gate > gj) | ((gate == gj) & (blk_row < j))
        rank = jnp.sum(jnp.where(beats & is_past, 1.0, 0.0), axis=0, keepdims=True)
        keep = (rank < MOBA_TOPK) & (q_blk[0:1, :] > j)
        bias_rows.append(jnp.where(keep, 0.0, MASK_VALUE))
    pad_rows = jnp.full((LANES - nb, S), 0.0, F32)
    biasT = jnp.concatenate(bias_rows + [pad_rows], axis=0).astype(BF16)

    er = lax.broadcasted_iota(jnp.int32, (blk, blk), 0)
    ec = lax.broadcasted_iota(jnp.int32, (blk, blk), 1)
    eye = (er == ec).astype(BF16)
    own_ok = ec <= er

    for i in range(nb):
        rows = slice(i * blk, (i + 1) * blk)
        nk = (i + 1) * blk
        s = _dot_nt(q[rows], k[:nk])
        bias_q = _dot_nt(eye, biasT[:, rows])
        parts = [s[:, j * blk:(j + 1) * blk] + bias_q[:, j:j + 1] for j in range(i)]
        parts.append(jnp.where(own_ok, s[:, i * blk:], MASK_VALUE))
        s = jnp.concatenate(parts, axis=1) if i else parts[0]
        m = jnp.max(s, axis=1, keepdims=True)
        p = jnp.exp(s - m)
        l = jnp.sum(p, axis=1, keepdims=True)
        o = _dot(p.astype(BF16), v[:nk]) / l
        o_ref[rows, :] = _rms(o, gain).astype(o_ref.dtype)


def _moba(mq, mk, mv, gain, batch, heads):
    T, width = mq.shape
    dh = width // heads
    S = T // batch
    spec = pl.BlockSpec((S, dh), lambda b, h: (b, h))
    return pl.pallas_call(
        _moba_kernel,
        grid=(batch, heads),
        in_specs=[spec, spec, spec, _resident(gain.shape)],
        out_specs=spec,
        out_shape=jax.ShapeDtypeStruct((T, width), BF16),
        compiler_params=pltpu.CompilerParams(dimension_semantics=("arbitrary", "arbitrary"),
                                             vmem_limit_bytes=VMEM_LIMIT_BYTES),
        name="moba",
    )(mq, mk, mv, gain)


def _out_ffn_kernel(apply_final, x_ref, og_ref, om_ref, wo_ref, g2_ref, wg_ref, wu_ref, wd_ref,
                    gf_ref, y_ref):
    n_gla = og_ref.shape[1]
    x1 = x_ref[...] + _dot(og_ref[...], wo_ref[:n_gla, :]) + _dot(om_ref[...], wo_ref[n_gla:, :])
    h = _rms(x1, g2_ref[...]).astype(BF16)
    d_ff = wg_ref.shape[1]
    ffn = None
    for lo in range(0, d_ff, FFN_CHUNK):
        width = min(FFN_CHUNK, d_ff - lo)
        g = _dot(h, wg_ref[:, lo:lo + width])
        u = _dot(h, wu_ref[:, lo:lo + width])
        act = (g / (1.0 + jnp.exp(-g)) * u).astype(BF16)
        part = _dot(act, wd_ref[lo:lo + width, :])
        ffn = part if ffn is None else ffn + part
    acc = x1 + ffn
    if apply_final:
        acc = _rms(acc, gf_ref[...])
    y_ref[...] = acc


def _out_ffn(x2d, o_gla, o_moba, w_out, g2, w_gate, w_up, w_down, g_final, apply_final):
    T, D = x2d.shape
    tm = TOKEN_TILE
    row = lambda width: pl.BlockSpec((tm, width), lambda i: (i, 0))
    return pl.pallas_call(
        functools.partial(_out_ffn_kernel, apply_final),
        grid=(T // tm,),
        in_specs=[row(D), row(o_gla.shape[1]), row(o_moba.shape[1]), _resident(w_out.shape),
                  _resident(g2.shape), _resident(w_gate.shape), _resident(w_up.shape),
                  _resident(w_down.shape), _resident(g_final.shape)],
        out_specs=row(D),
        out_shape=jax.ShapeDtypeStruct((T, D), F32),
        compiler_params=pltpu.CompilerParams(dimension_semantics=("arbitrary",),
                                             vmem_limit_bytes=VMEM_LIMIT_BYTES),
        name="out_ffn",
    )(x2d, o_gla, o_moba, w_out, g2, w_gate, w_up, w_down, g_final)


def kernel(x, positions, norm1_gain, w_in, w_gate_up, b_gate, gla_out_gain, moba_out_gain,
           w_out, norm2_gain, w_ffn_gate, w_ffn_up, w_ffn_down, final_norm_gain):
    B, S, D = x.shape
    depth = w_in.shape[0]
    gla_v = D // 2
    gla_dv = gla_v // GLA_HEADS
    gla_dk = gla_dv // 2
    gla_qk = GLA_HEADS * gla_dk
    moba_w = D - gla_v
    moba_dh = moba_w // MOBA_HEADS
    assert S % MOBA_BLOCK == 0 and S % GLA_KERNEL_CHUNK == 0 and (B * S) % TOKEN_TILE == 0
    assert moba_dh == LANES and gla_dv == LANES and GLA_GATE_RANK <= LANES

    sizes = (gla_qk, gla_qk, gla_v, gla_v, GLA_GATE_RANK, moba_w, moba_w, moba_w)
    offs = np.cumsum((0,) + sizes)
    seg = lambda w, i: w[:, offs[i]:offs[i + 1]]

    half = moba_dh // 2
    inv_freq = jnp.power(ROPE_THETA, -jnp.arange(half, dtype=F32) / half)
    inv_freq = jnp.concatenate([inv_freq, inv_freq])[None, :]

    T = B * S
    x2d = x.reshape(T, D)
    pos2d = positions.reshape(T, 1)
    pad_rank = LANES - GLA_GATE_RANK

    for l in range(depth):
        w = w_in[l]
        w_tok = jnp.concatenate(
            [seg(w, 0), seg(w, 2), seg(w, 3), seg(w, 5), seg(w, 6), seg(w, 7),
             jnp.pad(seg(w, 4), ((0, 0), (0, pad_rank)))], axis=1).astype(BF16)
        w_kT = seg(w, 1).T.astype(BF16)
        w_glT = jnp.pad(seg(w, 4), ((0, 0), (0, pad_rank))).T.astype(BF16)
        w_gu = jnp.pad(w_gate_up[l], ((0, pad_rank), (0, 0))).astype(BF16)
        w_guT = w_gu.T
        b_g = b_gate[l][None, :]
        b_gT = b_gate[l][:, None]

        q, kT, v, r, la, laT, mq, mk, mv = _in_proj(
            x2d, pos2d, norm1_gain[l][None, :], w_tok, w_kT, w_glT, w_gu, w_guT, b_g, b_gT,
            inv_freq, (gla_qk, gla_v, moba_w, moba_dh))
        o_gla = _gla(q, kT, v, la, laT, r, gla_out_gain[l][None, :], B, (GLA_HEADS, gla_dk, gla_dv))
        o_moba = _moba(mq, mk, mv, moba_out_gain[l][None, :], B, MOBA_HEADS)
        x2d = _out_ffn(x2d, o_gla, o_moba, w_out[l].astype(BF16), norm2_gain[l][None, :],
                       w_ffn_gate[l].astype(BF16), w_ffn_up[l].astype(BF16),
                       w_ffn_down[l].astype(BF16), final_norm_gain[None, :], l == depth - 1)
    return x2d.reshape(B, S, D)
```

```python
import functools

import jax
import jax.numpy as jnp
import numpy as np
from jax import lax
from jax.experimental import pallas as pl
from jax.experimental.pallas import tpu as pltpu

GLA_HEADS = 4
GLA_GATE_RANK = 16
GLA_GATE_TAU = 16.0
MOBA_HEADS = 4
MOBA_BLOCK = 256
MOBA_TOPK = 3
ROPE_THETA = 10000.0
EPS = 1e-6

LANES = 128
VMEM_LIMIT_BYTES = 56 * 1024 * 1024

GLA_KERNEL_CHUNK = 128
GLA_UNROLL = 4
MOBA_HEADS_PER_STEP = 2
TOKEN_TILE = 512
FFN_CHUNK = 512

MASK_VALUE = -1e30
LOG2_E = 1.4426950408889634

F32 = jnp.float32
BF16 = jnp.bfloat16


def _dot(a, b, precision=None):
    return jnp.dot(a, b, preferred_element_type=F32, precision=precision)


def _dot_nt(a, b):
    return lax.dot_general(a, b, (((1,), (1,)), ((), ())), preferred_element_type=F32)


def _rms(x, gain):
    ms = jnp.mean(x * x, axis=-1, keepdims=True)
    return x * lax.rsqrt(ms + EPS) * gain


def _resident(shape):
    nd = len(shape)
    return pl.BlockSpec(shape, lambda *_: (0,) * nd, pipeline_mode=pl.Buffered(1))


def _in_proj_kernel(dims, x_ref, pos_ref, g1_ref, w_ref, wgu_ref, bg_ref, invf_ref,
                    q_ref, k_ref, v_ref, r_ref, la_ref, mq_ref, mk_ref, mv_ref):
    gla_qk, gla_v, moba_w, moba_dh = dims
    tm = x_ref.shape[0]
    h = _rms(x_ref[...], g1_ref[...]).astype(BF16)

    def proj(lo, width):
        return _dot(h, w_ref[:, lo:lo + width])

    off = 0
    gla_dk = gla_qk // GLA_HEADS
    q_ref[...] = (proj(off, gla_qk) * gla_dk ** -0.5).astype(BF16)
    off += gla_qk
    k_ref[...] = proj(off, gla_qk).astype(BF16)
    off += gla_qk
    v_ref[...] = proj(off, gla_v).astype(BF16)
    off += gla_v
    gr = proj(off, gla_v)
    r_ref[...] = (gr / (1.0 + jnp.exp(-gr))).astype(BF16)
    off += gla_v
    mq = proj(off, moba_w)
    off += moba_w
    mk = proj(off, moba_w)
    off += moba_w
    mv_ref[...] = proj(off, moba_w).astype(BF16)
    off += moba_w
    g_low = proj(off, LANES).astype(BF16)

    logit = _dot(g_low, wgu_ref[...]) + bg_ref[...]
    log_sig = jnp.minimum(logit, 0.0) - jnp.log1p(jnp.exp(-jnp.abs(logit)))
    la_ref[...] = log_sig / GLA_GATE_TAU

    half = moba_dh // 2
    lane = lax.broadcasted_iota(jnp.int32, (tm // 2, moba_dh), 1)
    low = lane < half
    pos = pos_ref[...].astype(F32)
    ang = jnp.where(low, pos[:tm // 2], pos[tm // 2:]) * invf_ref[...]
    cos2 = jnp.cos(ang)
    sin2 = jnp.sin(ang)
    cos2_r = pltpu.roll(cos2, half, 1)
    sin2_r = pltpu.roll(sin2, half, 1)
    cos_sin = ((jnp.where(low, cos2, cos2_r), jnp.where(low, -sin2, sin2_r)),
               (jnp.where(low, cos2_r, cos2), jnp.where(low, -sin2_r, sin2)))
    scale = moba_dh ** -0.5 * LOG2_E
    for part, (cos, sin_signed) in enumerate(cos_sin):
        rows = slice(part * (tm // 2), (part + 1) * (tm // 2))
        for hh in range(moba_w // moba_dh):
            sl = slice(hh * moba_dh, (hh + 1) * moba_dh)
            xq = mq[rows, sl]
            xk = mk[rows, sl]
            rq = xq * cos + pltpu.roll(xq, half, 1) * sin_signed
            rk = xk * cos + pltpu.roll(xk, half, 1) * sin_signed
            mq_ref[rows, sl] = (rq * scale).astype(BF16)
            mk_ref[rows, sl] = rk.astype(BF16)


def _in_proj(x2d, pos2d, g1, w_tok, w_gu, b_g, inv_freq, dims):
    gla_qk, gla_v, moba_w, moba_dh = dims
    T, D = x2d.shape
    tm = TOKEN_TILE
    row = lambda width: pl.BlockSpec((tm, width), lambda i: (i, 0))
    out_shapes = (
        jax.ShapeDtypeStruct((T, gla_qk), BF16),
        jax.ShapeDtypeStruct((T, gla_qk), BF16),
        jax.ShapeDtypeStruct((T, gla_v), BF16),
        jax.ShapeDtypeStruct((T, gla_v), BF16),
        jax.ShapeDtypeStruct((T, gla_qk), F32),
        jax.ShapeDtypeStruct((T, moba_w), BF16),
        jax.ShapeDtypeStruct((T, moba_w), BF16),
        jax.ShapeDtypeStruct((T, moba_w), BF16),
    )
    out_specs = (row(gla_qk), row(gla_qk), row(gla_v), row(gla_v), row(gla_qk),
                 row(moba_w), row(moba_w), row(moba_w))
    in_specs = [row(D), row(1), _resident(g1.shape), _resident(w_tok.shape), _resident(w_gu.shape),
                _resident(b_g.shape), _resident(inv_freq.shape)]
    return pl.pallas_call(
        functools.partial(_in_proj_kernel, dims),
        grid=(T // tm,),
        in_specs=in_specs,
        out_specs=out_specs,
        out_shape=out_shapes,
        compiler_params=pltpu.CompilerParams(dimension_semantics=("arbitrary",),
                                             vmem_limit_bytes=VMEM_LIMIT_BYTES),
        name="in_proj",
    )(x2d, pos2d, g1, w_tok, w_gu, b_g, inv_freq)


def _gla_kernel(dims, q_ref, k_ref, v_ref, la_ref, r_ref, gain_ref, o_ref,
                state_ref, state_bf_ref, kz_ref):
    heads, dk, dv = dims
    S = q_ref.shape[0]
    C = GLA_KERNEL_CHUNK
    mid = C // 2 - 1

    r_i = lax.broadcasted_iota(jnp.int32, (C, C), 0)
    c_i = lax.broadcasted_iota(jnp.int32, (C, C), 1)
    causal = r_i >= c_i
    tril = causal.astype(BF16)

    state_ref[...] = jnp.zeros_like(state_ref)
    state_bf_ref[...] = jnp.zeros_like(state_bf_ref)
    kz_ref[...] = jnp.zeros_like(kz_ref)
    gain = gain_ref[...]

    def split(a):
        a_hi = a.astype(BF16)
        return a_hi, (a - a_hi.astype(F32)).astype(BF16)

    def step(n, carry):
        r0 = pl.multiple_of(n * C, C)
        la_hi, la_lo = split(la_ref[pl.ds(r0, C), :])
        b = _dot(tril, la_hi) + _dot(tril, la_lo)
        bT = b.T
        b_mid = b[mid:mid + 1, :]
        bT_mid = bT[:, mid:mid + 1]
        bT_last = bT[:, C - 1:C]

        q = q_ref[pl.ds(r0, C), :].astype(F32)
        kT = k_ref[pl.ds(r0, C), :].astype(F32).T
        q_inter = (q * jnp.exp(b)).astype(BF16)
        q_intra = (q * jnp.exp(b - b_mid)).astype(BF16)
        k_intraT = (kT * jnp.exp(bT_mid - bT)).astype(BF16)
        k_endT = (kT * jnp.exp(bT_last - bT)).astype(BF16)
        decay = jnp.exp(bT_last)

        o_inter = _dot(q_inter, state_bf_ref[...])
        for hh in range(heads):
            kr = slice(hh * dk, (hh + 1) * dk)
            vc = slice(hh * dv, (hh + 1) * dv)
            kz_ref[hh, kr, :] = k_intraT[kr, :]
            a = _dot(q_intra, kz_ref[hh])
            a = jnp.where(causal, a, 0.0).astype(BF16)
            vh = v_ref[pl.ds(r0, C), vc]
            oh = o_inter[:, vc] + _dot(a, vh)
            y = _rms(oh, gain) * r_ref[pl.ds(r0, C), vc].astype(F32)
            o_ref[pl.ds(r0, C), vc] = y.astype(o_ref.dtype)
            s_new = decay[kr, :] * state_ref[hh] + _dot(k_endT[kr, :], vh)
            state_ref[hh] = s_new
            state_bf_ref[kr, vc] = s_new.astype(BF16)
        return carry

    lax.fori_loop(0, S // C, step, 0, unroll=GLA_UNROLL)


def _gla(q, k, v, la, r, gain, batch, dims):
    heads, dk, dv = dims
    T = q.shape[0]
    S = T // batch
    row = lambda width: pl.BlockSpec((S, width), lambda b: (b, 0))
    return pl.pallas_call(
        functools.partial(_gla_kernel, dims),
        grid=(batch,),
        in_specs=[row(heads * dk), row(heads * dk), row(heads * dv), row(heads * dk),
                  row(heads * dv), _resident(gain.shape)],
        out_specs=row(heads * dv),
        out_shape=jax.ShapeDtypeStruct((T, heads * dv), BF16),
        scratch_shapes=[pltpu.VMEM((heads, dk, dv), F32),
                        pltpu.VMEM((heads * dk, heads * dv), BF16),
                        pltpu.VMEM((heads, heads * dk, GLA_KERNEL_CHUNK), BF16)],
        compiler_params=pltpu.CompilerParams(dimension_semantics=("arbitrary",),
                                             vmem_limit_bytes=VMEM_LIMIT_BYTES),
        name="gla",
    )(q, k, v, la, r, gain)


def _moba_head(q_ref, k_ref, v_ref, gain, o_ref, qx_ref, kx_ref, vx_ref):
    S, dh = q_ref.shape
    blk = MOBA_BLOCK
    nb = S // blk
    q = q_ref[...]
    k = k_ref[...]

    key_blk = lax.broadcasted_iota(jnp.int32, (S, dh), 0) // blk
    lane_id = lax.broadcasted_iota(jnp.int32, (S, dh), 1)
    qx_ref[:, :dh] = q
    kx_ref[:, :dh] = k
    kx_ref[:, dh:] = (key_blk == lane_id).astype(BF16)
    vx_ref[:, :dh] = v_ref[...]
    vx_ref[:, dh:] = jnp.ones((S, dh), BF16)

    k_mean = jnp.concatenate(
        [jnp.sum(k[j * blk:(j + 1) * blk].astype(F32), axis=0, keepdims=True) for j in range(nb)],
        axis=0) * (1.0 / blk)
    km_hi = k_mean.astype(BF16)
    km_lo = (k_mean - km_hi.astype(F32)).astype(BF16)
    g2 = _dot_nt(jnp.concatenate([km_hi, km_lo], axis=0), q)
    gate = g2[:nb] + g2[nb:]

    blk_row = lax.broadcasted_iota(jnp.int32, (nb, S), 0)
    q_blk = lax.broadcasted_iota(jnp.int32, (nb, S), 1) // blk
    is_past = blk_row < q_blk
    bias_rows = []
    for j in range(nb):
        gj = gate[j:j + 1, :]
        beats = (gate > gj) | ((gate == gj) & (blk_row < j))
        rank = jnp.sum(jnp.where(beats & is_past, 1.0, 0.0), axis=0, keepdims=True)
        keep = ((rank < MOBA_TOPK) & (q_blk[0:1, :] > j)) | (q_blk[0:1, :] == j)
        bias_rows.append(jnp.where(keep, 0.0, MASK_VALUE))
    pad_rows = jnp.full((LANES - nb, S), 0.0, F32)
    biasT = jnp.concatenate(bias_rows + [pad_rows], axis=0).astype(BF16)

    er = lax.broadcasted_iota(jnp.int32, (blk, blk), 0)
    ec = lax.broadcasted_iota(jnp.int32, (blk, blk), 1)
    eye = (er == ec).astype(BF16)
    own_ok = ec <= er

    no_select = [i for i in range(nb) if i <= MOBA_TOPK]
    for i in no_select + [i for i in reversed(range(nb)) if i > MOBA_TOPK]:
        rows = slice(i * blk, (i + 1) * blk)
        nk = (i + 1) * blk
        if i > MOBA_TOPK:
            qx_ref[rows, dh:] = _dot_nt(eye, biasT[:, rows]).astype(BF16)
        else:
            qx_ref[rows, dh:] = jnp.zeros((blk, dh), BF16)
        s = _dot_nt(qx_ref[rows, :], kx_ref[:nk, :])
        own = jnp.where(own_ok, s[:, i * blk:], MASK_VALUE)
        s = jnp.concatenate([s[:, :i * blk], own], axis=1) if i else own
        m = jnp.max(s, axis=1, keepdims=True)
        p = jnp.exp2(s - m).astype(BF16)
        ox = _dot(p, vx_ref[:nk, :])
        o = ox[:, :dh] / ox[:, dh:]
        o_ref[rows, :] = _rms(o, gain).astype(o_ref.dtype)


def _moba_kernel(q_ref, k_ref, v_ref, gain_ref, o_ref, qx_ref, kx_ref, vx_ref):
    dh = gain_ref.shape[1]
    gain = gain_ref[...]
    for g in range(q_ref.shape[1] // dh):
        cols = pl.ds(g * dh, dh)
        _moba_head(q_ref.at[:, cols], k_ref.at[:, cols], v_ref.at[:, cols], gain, o_ref.at[:, cols],
                   qx_ref.at[g], kx_ref.at[g], vx_ref.at[g])


def _moba(mq, mk, mv, gain, batch, heads):
    T, width = mq.shape
    dh = width // heads
    S = T // batch
    group = MOBA_HEADS_PER_STEP
    spec = pl.BlockSpec((S, group * dh), lambda b, h: (b, h))
    return pl.pallas_call(
        _moba_kernel,
        grid=(batch, heads // group),
        in_specs=[spec, spec, spec, _resident(gain.shape)],
        out_specs=spec,
        out_shape=jax.ShapeDtypeStruct((T, width), BF16),
        scratch_shapes=[pltpu.VMEM((group, S, 2 * dh), BF16)] * 3,
        compiler_params=pltpu.CompilerParams(dimension_semantics=("arbitrary", "arbitrary"),
                                             vmem_limit_bytes=VMEM_LIMIT_BYTES),
        name="moba",
    )(mq, mk, mv, gain)


def _out_ffn_kernel(apply_final, x_ref, og_ref, om_ref, wo_ref, g2_ref, wg_ref, wu_ref, wd_ref,
                    gf_ref, y_ref):
    n_gla = og_ref.shape[1]
    x1 = x_ref[...] + _dot(og_ref[...], wo_ref[:n_gla, :]) + _dot(om_ref[...], wo_ref[n_gla:, :])
    h = _rms(x1, g2_ref[...]).astype(BF16)
    d_ff = wg_ref.shape[1]
    ffn = None
    for lo in range(0, d_ff, FFN_CHUNK):
        width = min(FFN_CHUNK, d_ff - lo)
        g = _dot(h, wg_ref[:, lo:lo + width])
        u = _dot(h, wu_ref[:, lo:lo + width])
        act = (g / (1.0 + jnp.exp(-g)) * u).astype(BF16)
        part = _dot(act, wd_ref[lo:lo + width, :])
        ffn = part if ffn is None else ffn + part
    acc = x1 + ffn
    if apply_final:
        acc = _rms(acc, gf_ref[...])
    y_ref[...] = acc


def _out_ffn(x2d, o_gla, o_moba, w_out, g2, w_gate, w_up, w_down, g_final, apply_final):
    T, D = x2d.shape
    tm = TOKEN_TILE
    row = lambda width: pl.BlockSpec((tm, width), lambda i: (i, 0))
    return pl.pallas_call(
        functools.partial(_out_ffn_kernel, apply_final),
        grid=(T // tm,),
        in_specs=[row(D), row(o_gla.shape[1]), row(o_moba.shape[1]), _resident(w_out.shape),
                  _resident(g2.shape), _resident(w_gate.shape), _resident(w_up.shape),
                  _resident(w_down.shape), _resident(g_final.shape)],
        out_specs=row(D),
        out_shape=jax.ShapeDtypeStruct((T, D), F32),
        compiler_params=pltpu.CompilerParams(dimension_semantics=("arbitrary",),
                                             vmem_limit_bytes=VMEM_LIMIT_BYTES),
        name="out_ffn",
    )(x2d, o_gla, o_moba, w_out, g2, w_gate, w_up, w_down, g_final)


def kernel(x, positions, norm1_gain, w_in, w_gate_up, b_gate, gla_out_gain, moba_out_gain,
           w_out, norm2_gain, w_ffn_gate, w_ffn_up, w_ffn_down, final_norm_gain):
    B, S, D = x.shape
    depth = w_in.shape[0]
    gla_v = D // 2
    gla_dv = gla_v // GLA_HEADS
    gla_dk = gla_dv // 2
    gla_qk = GLA_HEADS * gla_dk
    moba_w = D - gla_v
    moba_dh = moba_w // MOBA_HEADS
    assert S % MOBA_BLOCK == 0 and S % GLA_KERNEL_CHUNK == 0 and (B * S) % TOKEN_TILE == 0
    assert moba_dh == LANES and gla_dv == LANES and GLA_GATE_RANK <= LANES

    sizes = (gla_qk, gla_qk, gla_v, gla_v, GLA_GATE_RANK, moba_w, moba_w, moba_w)
    offs = np.cumsum((0,) + sizes)
    seg = lambda w, i: w[:, offs[i]:offs[i + 1]]

    half = moba_dh // 2
    inv_freq = jnp.power(ROPE_THETA, -jnp.arange(half, dtype=F32) / half)
    inv_freq = jnp.concatenate([inv_freq, inv_freq])[None, :]

    T = B * S
    x2d = x.reshape(T, D)
    pos2d = positions.reshape(T, 1)
    pad_rank = LANES - GLA_GATE_RANK

    for l in range(depth):
        w = w_in[l]
        w_tok = jnp.concatenate(
            [seg(w, 0), seg(w, 1), seg(w, 2), seg(w, 3), seg(w, 5), seg(w, 6), seg(w, 7),
             jnp.pad(seg(w, 4), ((0, 0), (0, pad_rank)))], axis=1).astype(BF16)
        w_gu = jnp.pad(w_gate_up[l], ((0, pad_rank), (0, 0))).astype(BF16)

        q, k, v, r, la, mq, mk, mv = _in_proj(
            x2d, pos2d, norm1_gain[l][None, :], w_tok, w_gu, b_gate[l][None, :], inv_freq,
            (gla_qk, gla_v, moba_w, moba_dh))
        o_gla = _gla(q, k, v, la, r, gla_out_gain[l][None, :], B, (GLA_HEADS, gla_dk, gla_dv))
        o_moba = _moba(mq, mk, mv, moba_out_gain[l][None, :], B, MOBA_HEADS)
        x2d = _out_ffn(x2d, o_gla, o_moba, w_out[l].astype(BF16), norm2_gain[l][None, :],
                       w_ffn_gate[l].astype(BF16), w_ffn_up[l].astype(BF16),
                       w_ffn_down[l].astype(BF16), final_norm_gain[None, :], l == depth - 1)
    return x2d.reshape(B, S, D)
```

```python
import functools

import jax
import jax.numpy as jnp
import numpy as np
from jax import lax
from jax.experimental import pallas as pl
from jax.experimental.pallas import tpu as pltpu

GLA_HEADS = 4
GLA_GATE_RANK = 16
GLA_GATE_TAU = 16.0
MOBA_HEADS = 4
MOBA_BLOCK = 256
MOBA_TOPK = 3
ROPE_THETA = 10000.0
EPS = 1e-6

LANES = 128
VMEM_LIMIT_BYTES = 56 * 1024 * 1024

GLA_KERNEL_CHUNK = 128
MOBA_HEADS_PER_STEP = 2
TOKEN_TILE = 512
FFN_CHUNK = 512

MASK_VALUE = -1e30
LOG2_E = 1.4426950408889634

F32 = jnp.float32
BF16 = jnp.bfloat16


def _dot(a, b, precision=None):
    return jnp.dot(a, b, preferred_element_type=F32, precision=precision)


def _dot_nt(a, b):
    return lax.dot_general(a, b, (((1,), (1,)), ((), ())), preferred_element_type=F32)


def _rms(x, gain):
    ms = jnp.mean(x * x, axis=-1, keepdims=True)
    return x * lax.rsqrt(ms + EPS) * gain


def _resident(shape):
    nd = len(shape)
    return pl.BlockSpec(shape, lambda *_: (0,) * nd, pipeline_mode=pl.Buffered(1))


def _in_proj_kernel(dims, x_ref, pos_ref, g1_ref, w_ref, wgu_ref, bg_ref, invf_ref,
                    q_ref, k_ref, v_ref, r_ref, la_ref, mq_ref, mk_ref, mv_ref):
    gla_qk, gla_v, moba_w, moba_dh = dims
    tm = x_ref.shape[0]
    names = ("gq", "gk", "gv", "gr", "mq", "mk", "mv", "g_low")
    widths = (gla_qk, gla_qk, gla_v, gla_v, moba_w, moba_w, moba_w, LANES)
    col = {name: sum(widths[:i]) for i, name in enumerate(names)}
    h = _rms(x_ref[...], g1_ref[...]).astype(BF16)

    def proj(name):
        lo = col[name]
        return _dot(h, w_ref[:, lo:lo + widths[names.index(name)]])

    g_low = proj("g_low").astype(BF16)
    mq = proj("mq")

    logit = _dot(g_low, wgu_ref[...]) + bg_ref[...]
    log_sig = jnp.minimum(logit, 0.0) - jnp.log1p(jnp.exp(-jnp.abs(logit)))
    la_ref[...] = log_sig / GLA_GATE_TAU

    mk = proj("mk")

    half = moba_dh // 2
    lane = lax.broadcasted_iota(jnp.int32, (tm // 2, moba_dh), 1)
    low = lane < half
    pos = pos_ref[...].astype(F32)
    ang = jnp.where(low, pos[:tm // 2], pos[tm // 2:]) * invf_ref[...]
    cos2 = jnp.cos(ang)
    sin2 = jnp.sin(ang)
    cos2_r = pltpu.roll(cos2, half, 1)
    sin2_r = pltpu.roll(sin2, half, 1)
    cos_sin = ((jnp.where(low, cos2, cos2_r), jnp.where(low, -sin2, sin2_r)),
               (jnp.where(low, cos2_r, cos2), jnp.where(low, -sin2_r, sin2)))
    scale = moba_dh ** -0.5 * LOG2_E

    def rotary(part):
        cos, sin_signed = cos_sin[part]
        rows = slice(part * (tm // 2), (part + 1) * (tm // 2))
        for hh in range(moba_w // moba_dh):
            sl = slice(hh * moba_dh, (hh + 1) * moba_dh)
            xq = mq[rows, sl]
            xk = mk[rows, sl]
            rq = xq * cos + pltpu.roll(xq, half, 1) * sin_signed
            rk = xk * cos + pltpu.roll(xk, half, 1) * sin_signed
            mq_ref[rows, sl] = (rq * scale).astype(BF16)
            mk_ref[rows, sl] = rk.astype(BF16)

    v_ref[...] = proj("gv").astype(BF16)
    rotary(0)
    mv_ref[...] = proj("mv").astype(BF16)
    rotary(1)
    gr = proj("gr")
    r_ref[...] = (gr / (1.0 + jnp.exp(-gr))).astype(BF16)
    gla_dk = gla_qk // GLA_HEADS
    q_ref[...] = (proj("gq") * gla_dk ** -0.5).astype(BF16)
    k_ref[...] = proj("gk").astype(BF16)


def _in_proj(x2d, pos2d, g1, w_tok, w_gu, b_g, inv_freq, dims):
    gla_qk, gla_v, moba_w, moba_dh = dims
    T, D = x2d.shape
    tm = TOKEN_TILE
    row = lambda width: pl.BlockSpec((tm, width), lambda i: (i, 0))
    out_shapes = (
        jax.ShapeDtypeStruct((T, gla_qk), BF16),
        jax.ShapeDtypeStruct((T, gla_qk), BF16),
        jax.ShapeDtypeStruct((T, gla_v), BF16),
        jax.ShapeDtypeStruct((T, gla_v), BF16),
        jax.ShapeDtypeStruct((T, gla_qk), F32),
        jax.ShapeDtypeStruct((T, moba_w), BF16),
        jax.ShapeDtypeStruct((T, moba_w), BF16),
        jax.ShapeDtypeStruct((T, moba_w), BF16),
    )
    out_specs = (row(gla_qk), row(gla_qk), row(gla_v), row(gla_v), row(gla_qk),
                 row(moba_w), row(moba_w), row(moba_w))
    in_specs = [row(D), row(1), _resident(g1.shape), _resident(w_tok.shape), _resident(w_gu.shape),
                _resident(b_g.shape), _resident(inv_freq.shape)]
    return pl.pallas_call(
        functools.partial(_in_proj_kernel, dims),
        grid=(T // tm,),
        in_specs=in_specs,
        out_specs=out_specs,
        out_shape=out_shapes,
        compiler_params=pltpu.CompilerParams(dimension_semantics=("arbitrary",),
                                             vmem_limit_bytes=VMEM_LIMIT_BYTES),
        name="in_proj",
    )(x2d, pos2d, g1, w_tok, w_gu, b_g, inv_freq)


def _gla_kernel(dims, q_ref, k_ref, v_ref, la_ref, r_ref, gain_ref, o_ref):
    heads, dk, dv = dims
    S = q_ref.shape[0]
    C = GLA_KERNEL_CHUNK
    mid = C // 2 - 1
    n_chunks = S // C

    r_i = lax.broadcasted_iota(jnp.int32, (C, C), 0)
    c_i = lax.broadcasted_iota(jnp.int32, (C, C), 1)
    causal = r_i >= c_i
    tril = causal.astype(BF16)

    gain = gain_ref[...]
    lane_head = lax.broadcasted_iota(jnp.int32, (C, heads * dk), 1) // dk

    def split(a):
        a_hi = a.astype(BF16)
        return a_hi, (a - a_hi.astype(F32)).astype(BF16)

    def local(n):
        rows = slice(n * C, (n + 1) * C)
        la_hi, la_lo = split(la_ref[rows, :])
        b = _dot(tril, la_hi) + _dot(tril, la_lo)
        b_mid = b[mid:mid + 1, :]
        b_last = b[C - 1:C, :]

        q = q_ref[rows, :].astype(F32)
        k = k_ref[rows, :].astype(F32)
        q_inter = (q * jnp.exp(b)).astype(BF16)
        q_intra = (q * jnp.exp(b - b_mid)).astype(BF16)
        k_intra = (k * jnp.exp(b_mid - b)).astype(BF16)
        k_endT = (k * jnp.exp(b_last - b)).T.astype(BF16)
        decay = jnp.exp(jnp.broadcast_to(b_last, (dv, heads * dk)).T)
        return q_inter, q_intra, k_intra, k_endT, decay

    def intra(n, q_intra, k_intra, k_endT):
        rows = slice(n * C, (n + 1) * C)
        o_intra, u = [], []
        for hh in range(heads):
            kr = slice(hh * dk, (hh + 1) * dk)
            kz = jnp.where(lane_head == hh, k_intra, jnp.zeros_like(k_intra))
            a = _dot_nt(q_intra, kz)
            a = jnp.where(causal, a, 0.0).astype(BF16)
            vh = v_ref[rows, hh * dv:(hh + 1) * dv]
            o_intra.append(_dot(a, vh))
            u.append(_dot(k_endT[kr, :], vh))
        return o_intra, u

    def recur(n, state, q_inter, o_intra, u, decay):
        rows = slice(n * C, (n + 1) * C)
        zero = jnp.zeros((dk, dv), BF16)
        state_bd = jnp.concatenate(
            [jnp.concatenate([state[hh].astype(BF16) if g == hh else zero for g in range(heads)], axis=1)
             for hh in range(heads)], axis=0)
        o_inter = _dot(q_inter, state_bd)
        new_state = []
        for hh in range(heads):
            kr = slice(hh * dk, (hh + 1) * dk)
            vc = slice(hh * dv, (hh + 1) * dv)
            y = _rms(o_inter[:, vc] + o_intra[hh], gain) * r_ref[rows, vc].astype(F32)
            o_ref[rows, vc] = y.astype(o_ref.dtype)
            new_state.append(decay[kr, :] * state[hh] + u[hh])
        return new_state

    staged = {}
    state = [jnp.zeros((dk, dv), F32)] * heads
    for t in range(n_chunks + 2):
        if t < n_chunks:
            staged[t] = local(t)
        if 0 <= t - 1 < n_chunks:
            q_inter, q_intra, k_intra, k_endT, decay = staged[t - 1]
            staged[t - 1] = (q_inter, decay) + intra(t - 1, q_intra, k_intra, k_endT)
        if 0 <= t - 2:
            q_inter, decay, o_intra, u = staged.pop(t - 2)
            state = recur(t - 2, state, q_inter, o_intra, u, decay)


def _gla(q, k, v, la, r, gain, batch, dims):
    heads, dk, dv = dims
    T = q.shape[0]
    S = T // batch
    row = lambda width: pl.BlockSpec((S, width), lambda b: (b, 0))
    return pl.pallas_call(
        functools.partial(_gla_kernel, dims),
        grid=(batch,),
        in_specs=[row(heads * dk), row(heads * dk), row(heads * dv), row(heads * dk),
                  row(heads * dv), _resident(gain.shape)],
        out_specs=row(heads * dv),
        out_shape=jax.ShapeDtypeStruct((T, heads * dv), BF16),
        compiler_params=pltpu.CompilerParams(dimension_semantics=("arbitrary",),
                                             vmem_limit_bytes=VMEM_LIMIT_BYTES),
        name="gla",
    )(q, k, v, la, r, gain)


class _MobaHead:
    def __init__(self, q_ref, k_ref, v_ref, gain, o_ref, qx_ref, kx_ref, vx_ref):
        self.refs = (q_ref, k_ref, v_ref, o_ref, qx_ref, kx_ref, vx_ref)
        self.gain = gain
        self.S, self.dh = q_ref.shape
        self.nb = self.S // MOBA_BLOCK
        self.biasT = None

    def widen(self):
        q_ref, k_ref, v_ref, _, qx_ref, kx_ref, vx_ref = self.refs
        S, dh, blk = self.S, self.dh, MOBA_BLOCK
        key_blk = lax.broadcasted_iota(jnp.int32, (S, dh), 0) // blk
        lane_id = lax.broadcasted_iota(jnp.int32, (S, dh), 1)
        qx_ref[:, :dh] = q_ref[...]
        kx_ref[:, :dh] = k_ref[...]
        kx_ref[:, dh:] = (key_blk == lane_id).astype(BF16)
        vx_ref[:, :dh] = v_ref[...]
        vx_ref[:, dh:] = jnp.ones((S, dh), BF16)

    def gate_scores(self):
        q_ref, k_ref = self.refs[:2]
        nb, blk = self.nb, MOBA_BLOCK
        q = q_ref[...]
        k = k_ref[...]
        k_mean = jnp.concatenate(
            [jnp.sum(k[j * blk:(j + 1) * blk].astype(F32), axis=0, keepdims=True) for j in range(nb)],
            axis=0) * (1.0 / blk)
        km_hi = k_mean.astype(BF16)
        km_lo = (k_mean - km_hi.astype(F32)).astype(BF16)
        g2 = _dot_nt(jnp.concatenate([km_hi, km_lo], axis=0), q)
        self.gate = g2[:nb] + g2[nb:]

    def select(self):
        S, nb, blk = self.S, self.nb, MOBA_BLOCK
        gate = self.gate
        blk_row = lax.broadcasted_iota(jnp.int32, (nb, S), 0)
        q_blk = lax.broadcasted_iota(jnp.int32, (nb, S), 1) // blk
        is_past = blk_row < q_blk
        bias_rows = []
        for j in range(nb):
            gj = gate[j:j + 1, :]
            beats = (gate > gj) | ((gate == gj) & (blk_row < j))
            rank = jnp.sum(jnp.where(beats & is_past, 1.0, 0.0), axis=0, keepdims=True)
            keep = ((rank < MOBA_TOPK) & (q_blk[0:1, :] > j)) | (q_blk[0:1, :] == j)
            bias_rows.append(jnp.where(keep, 0.0, MASK_VALUE))
        pad_rows = jnp.full((LANES - nb, S), 0.0, F32)
        self.biasT = jnp.concatenate(bias_rows + [pad_rows], axis=0).astype(BF16)

    def scores(self, i):
        qx_ref, kx_ref = self.refs[4:6]
        dh, blk = self.dh, MOBA_BLOCK
        rows = slice(i * blk, (i + 1) * blk)
        if i > MOBA_TOPK:
            er = lax.broadcasted_iota(jnp.int32, (blk, blk), 0)
            ec = lax.broadcasted_iota(jnp.int32, (blk, blk), 1)
            qx_ref[rows, dh:] = _dot_nt((er == ec).astype(BF16), self.biasT[:, rows]).astype(BF16)
        else:
            qx_ref[rows, dh:] = jnp.zeros((blk, dh), BF16)
        return _dot_nt(qx_ref[rows, :], kx_ref[:(i + 1) * blk, :])

    def attend(self, i, s):
        o_ref, vx_ref = self.refs[3], self.refs[6]
        dh, blk = self.dh, MOBA_BLOCK
        er = lax.broadcasted_iota(jnp.int32, (blk, blk), 0)
        ec = lax.broadcasted_iota(jnp.int32, (blk, blk), 1)
        own = jnp.where(ec <= er, s[:, i * blk:], MASK_VALUE)
        s = jnp.concatenate([s[:, :i * blk], own], axis=1) if i else own
        m = jnp.max(s, axis=1, keepdims=True)
        p = jnp.exp2(s - m).astype(BF16)
        ox = _dot(p, vx_ref[:(i + 1) * blk, :])
        o = ox[:, :dh] / ox[:, dh:]
        o_ref[i * blk:(i + 1) * blk, :] = _rms(o, self.gain).astype(o_ref.dtype)


def _moba_kernel(q_ref, k_ref, v_ref, gain_ref, o_ref, qx_ref, kx_ref, vx_ref):
    dh = gain_ref.shape[1]
    gain = gain_ref[...]
    heads = []
    for g in range(q_ref.shape[1] // dh):
        cols = pl.ds(g * dh, dh)
        heads.append(_MobaHead(q_ref.at[:, cols], k_ref.at[:, cols], v_ref.at[:, cols], gain,
                               o_ref.at[:, cols], qx_ref.at[g], kx_ref.at[g], vx_ref.at[g]))
    nb = heads[0].nb
    free = [i for i in range(nb) if i <= MOBA_TOPK]
    first, last = free[len(free) // 2:][::-1], free[:len(free) // 2][::-1]
    rest = [i for i in reversed(range(nb)) if i > MOBA_TOPK]
    tasks = [(hd, i) for i in first + rest + last for hd in heads]
    for hd in heads:
        hd.widen()
        hd.gate_scores()
    s_next = tasks[0][0].scores(tasks[0][1])
    for t, (hd, i) in enumerate(tasks):
        s_cur = s_next
        if t < len(heads):
            heads[t].select()
        if t + 1 < len(tasks):
            s_next = tasks[t + 1][0].scores(tasks[t + 1][1])
        hd.attend(i, s_cur)


def _moba(mq, mk, mv, gain, batch, heads):
    T, width = mq.shape
    dh = width // heads
    S = T // batch
    group = MOBA_HEADS_PER_STEP
    spec = pl.BlockSpec((S, group * dh), lambda b, h: (b, h))
    return pl.pallas_call(
        _moba_kernel,
        grid=(batch, heads // group),
        in_specs=[spec, spec, spec, _resident(gain.shape)],
        out_specs=spec,
        out_shape=jax.ShapeDtypeStruct((T, width), BF16),
        scratch_shapes=[pltpu.VMEM((group, S, 2 * dh), BF16)] * 3,
        compiler_params=pltpu.CompilerParams(dimension_semantics=("arbitrary", "arbitrary"),
                                             vmem_limit_bytes=VMEM_LIMIT_BYTES),
        name="moba",
    )(mq, mk, mv, gain)


def _out_ffn_kernel(apply_final, x_ref, og_ref, om_ref, wo_ref, g2_ref, wg_ref, wu_ref, wd_ref,
                    gf_ref, y_ref):
    n_gla = og_ref.shape[1]
    x1 = x_ref[...] + _dot(og_ref[...], wo_ref[:n_gla, :]) + _dot(om_ref[...], wo_ref[n_gla:, :])
    h = _rms(x1, g2_ref[...]).astype(BF16)
    d_ff = wg_ref.shape[1]
    ffn = None
    for lo in range(0, d_ff, FFN_CHUNK):
        width = min(FFN_CHUNK, d_ff - lo)
        g = _dot(h, wg_ref[:, lo:lo + width])
        u = _dot(h, wu_ref[:, lo:lo + width])
        act = (g / (1.0 + jnp.exp(-g)) * u).astype(BF16)
        part = _dot(act, wd_ref[lo:lo + width, :])
        ffn = part if ffn is None else ffn + part
    acc = x1 + ffn
    if apply_final:
        acc = _rms(acc, gf_ref[...])
    y_ref[...] = acc


def _out_ffn(x2d, o_gla, o_moba, w_out, g2, w_gate, w_up, w_down, g_final, apply_final):
    T, D = x2d.shape
    tm = TOKEN_TILE
    row = lambda width: pl.BlockSpec((tm, width), lambda i: (i, 0))
    return pl.pallas_call(
        functools.partial(_out_ffn_kernel, apply_final),
        grid=(T // tm,),
        in_specs=[row(D), row(o_gla.shape[1]), row(o_moba.shape[1]), _resident(w_out.shape),
                  _resident(g2.shape), _resident(w_gate.shape), _resident(w_up.shape),
                  _resident(w_down.shape), _resident(g_final.shape)],
        out_specs=row(D),
        out_shape=jax.ShapeDtypeStruct((T, D), F32),
        compiler_params=pltpu.CompilerParams(dimension_semantics=("arbitrary",),
                                             vmem_limit_bytes=VMEM_LIMIT_BYTES),
        name="out_ffn",
    )(x2d, o_gla, o_moba, w_out, g2, w_gate, w_up, w_down, g_final)


def kernel(x, positions, norm1_gain, w_in, w_gate_up, b_gate, gla_out_gain, moba_out_gain,
           w_out, norm2_gain, w_ffn_gate, w_ffn_up, w_ffn_down, final_norm_gain):
    B, S, D = x.shape
    depth = w_in.shape[0]
    gla_v = D // 2
    gla_dv = gla_v // GLA_HEADS
    gla_dk = gla_dv // 2
    gla_qk = GLA_HEADS * gla_dk
    moba_w = D - gla_v
    moba_dh = moba_w // MOBA_HEADS
    assert S % MOBA_BLOCK == 0 and S % GLA_KERNEL_CHUNK == 0 and (B * S) % TOKEN_TILE == 0
    assert moba_dh == LANES and gla_dv == LANES and GLA_GATE_RANK <= LANES

    sizes = (gla_qk, gla_qk, gla_v, gla_v, GLA_GATE_RANK, moba_w, moba_w, moba_w)
    offs = np.cumsum((0,) + sizes)
    seg = lambda w, i: w[:, offs[i]:offs[i + 1]]

    half = moba_dh // 2
    inv_freq = jnp.power(ROPE_THETA, -jnp.arange(half, dtype=F32) / half)
    inv_freq = jnp.concatenate([inv_freq, inv_freq])[None, :]

    T = B * S
    x2d = x.reshape(T, D)
    pos2d = positions.reshape(T, 1)
    pad_rank = LANES - GLA_GATE_RANK

    for l in range(depth):
        w = w_in[l]
        w_tok = jnp.concatenate(
            [seg(w, 0), seg(w, 1), seg(w, 2), seg(w, 3), seg(w, 5), seg(w, 6), seg(w, 7),
             jnp.pad(seg(w, 4), ((0, 0), (0, pad_rank)))], axis=1).astype(BF16)
        w_gu = jnp.pad(w_gate_up[l], ((0, pad_rank), (0, 0))).astype(BF16)

        q, k, v, r, la, mq, mk, mv = _in_proj(
            x2d, pos2d, norm1_gain[l][None, :], w_tok, w_gu, b_gate[l][None, :], inv_freq,
            (gla_qk, gla_v, moba_w, moba_dh))
        o_gla = _gla(q, k, v, la, r, gla_out_gain[l][None, :], B, (GLA_HEADS, gla_dk, gla_dv))
        o_moba = _moba(mq, mk, mv, moba_out_gain[l][None, :], B, MOBA_HEADS)
        x2d = _out_ffn(x2d, o_gla, o_moba, w_out[l].astype(BF16), norm2_gain[l][None, :],
                       w_ffn_gate[l].astype(BF16), w_ffn_up[l].astype(BF16),
                       w_ffn_down[l].astype(BF16), final_norm_gain[None, :], l == depth - 1)
    return x2d.reshape(B, S, D)
```
